```python
import numpy as np
import jax
import jax.numpy as jnp
from jax import lax

D_MODEL = 2048
BATCH = 4
SEQ = 2048
DEPTH = 4
DEC_BATCH = 128
DEC_SEQ = 1
PAST_LEN = 16384
PAGE_SIZE = 128

BRANCH_W = D_MODEL // 2
N_BRANCH = 3
DK_A = 128
DV_A = 128
HA = BRANCH_W // DV_A
CONV_A = 4
CONV_CH_A = 2 * HA * DK_A + HA * DV_A
CHUNK_A = 64
DK_B = 128
DV_B = 256
HB = BRANCH_W // DV_B
CHUNK_B = 64
ROPE_BASE = 10000.0
DK_C = 128
DV_C = 256
HC = BRANCH_W // DV_C
GLA_RANK = 16
GLA_TAU = 16.0
CHUNK_C = 16
D_FF = ((8 * D_MODEL // 3 + 255) // 256) * 256
FFN_CONV = 3
EPS = 1e-6

IN_SIZES = (CONV_CH_A, HA * DV_A, HA, HA,
            HB * DK_B, HB * DK_B, HB * DV_B, HB * DV_B,
            HC * DK_C, HC * DK_C, HC * DV_C, HC * DV_C, GLA_RANK,
            N_BRANCH * D_MODEL)
D_IN = sum(IN_SIZES)
SPLIT_POINTS = tuple(int(s) for s in np.cumsum(IN_SIZES)[:-1])

kernel_name = "hybrid_deltanet_retnet_gla_convffn_decode_step"


def _rmsnorm(x, g=None):
    xf = x.astype(jnp.float32)
    y = xf * lax.rsqrt(jnp.mean(xf * xf, axis=-1, keepdims=True) + EPS)
    return y if g is None else y * g.astype(jnp.float32)


def _l2norm(x):
    return x * lax.rsqrt(jnp.sum(x * x, axis=-1, keepdims=True) + EPS)


def _tril(c, k):
    return jnp.tril(jnp.ones((c, c), dtype=bool), k)


def _causal_dwconv(x, buf, w):
    n_tap = w.shape[0]
    length = x.shape[1]
    xp = jnp.concatenate([buf.astype(x.dtype), x], axis=1)
    y = xp[:, 0:length] * w[0]
    for i in range(1, n_tap):
        y = y + xp[:, i:i + length] * w[i]
    return y, xp[:, xp.shape[1] - (n_tap - 1):]


def _rope(x, pos):
    half = x.shape[-1] // 2
    inv = ROPE_BASE ** (-jnp.arange(half, dtype=jnp.float32) / half)
    ang = pos[:, None] * inv[None, :]
    cos = jnp.cos(ang)[None, :, None, :]
    sin = jnp.sin(ang)[None, :, None, :]
    x1, x2 = x[..., :half], x[..., half:]
    return jnp.concatenate([x1 * cos - x2 * sin, x2 * cos + x1 * sin], axis=-1)


def _to_chunks(x, chunk):
    bsz, length = x.shape[0], x.shape[1]
    pad = (-length) % chunk
    x = jnp.pad(x, [(0, 0), (0, pad)] + [(0, 0)] * (x.ndim - 2))
    n = (length + pad) // chunk
    x = x.reshape((bsz, n, chunk) + x.shape[2:])
    return x.transpose((1, 0, 3, 2) + tuple(range(4, x.ndim)))


def _from_chunks(o, length):
    n, bsz, nh, c, dv = o.shape
    return o.transpose(1, 0, 3, 2, 4).reshape(bsz, n * c, nh, dv)[:, :length]


def _chunk_gated_delta(q, k, v, beta, g, s0, chunk):
    length = q.shape[1]
    dv = v.shape[-1]
    q, k, v, beta, g = (_to_chunks(t, chunk) for t in (q, k, v, beta, g))
    b = jnp.cumsum(g, axis=-1)
    diff = b[..., :, None] - b[..., None, :]
    d_strict = jnp.exp(jnp.where(_tril(chunk, -1), diff, -jnp.inf))
    d_incl = jnp.exp(jnp.where(_tril(chunk, 0), diff, -jnp.inf))
    a_mat = jnp.eye(chunk, dtype=q.dtype) + beta[..., :, None] * jnp.einsum('nbhtd,nbhjd->nbhtj', k, k) * d_strict
    rhs = jnp.concatenate([beta[..., None] * v, (beta * jnp.exp(b))[..., None] * k], axis=-1)
    sol = lax.linalg.triangular_solve(a_mat, rhs, left_side=True, lower=True, unit_diagonal=True)
    u_v, w = sol[..., :dv], sol[..., dv:]
    qk = jnp.einsum('nbhtd,nbhjd->nbhtj', q, k) * d_incl
    q_in = q * jnp.exp(b)[..., None]
    k_out = k * jnp.exp(b[..., -1:] - b)[..., None]
    cd = jnp.exp(b[..., -1])

    def step(s, xs):
        uv_c, w_c, qk_c, qi_c, ko_c, cd_c = xs
        u = uv_c - jnp.einsum('bhtd,bhdv->bhtv', w_c, s)
        o = jnp.einsum('bhtd,bhdv->bhtv', qi_c, s) + jnp.einsum('bhtj,bhjv->bhtv', qk_c, u)
        s = s * cd_c[..., None, None] + jnp.einsum('bhtd,bhtv->bhdv', ko_c, u)
        return s, o

    s, o = lax.scan(step, s0, (u_v, w, qk, q_in, k_out, cd))
    return _from_chunks(o, length), s


def _chunk_scalar_decay(q, k, v, g, s0, chunk):
    length = q.shape[1]
    q, k, v, g = (_to_chunks(t, chunk) for t in (q, k, v, g))
    b = jnp.cumsum(g, axis=-1)
    dec = jnp.exp(jnp.where(_tril(chunk, 0), b[..., :, None] - b[..., None, :], -jnp.inf))
    o_intra = jnp.einsum('nbhtj,nbhjv->nbhtv', jnp.einsum('nbhtd,nbhjd->nbhtj', q, k) * dec, v)
    q_in = q * jnp.exp(b)[..., None]
    k_out = k * jnp.exp(b[..., -1:] - b)[..., None]
    cd = jnp.exp(b[..., -1])

    def step(s, xs):
        qi_c, ko_c, v_c, cd_c, oi_c = xs
        o = oi_c + jnp.einsum('bhtd,bhdv->bhtv', qi_c, s)
        s = s * cd_c[..., None, None] + jnp.einsum('bhtd,bhtv->bhdv', ko_c, v_c)
        return s, o

    s, o = lax.scan(step, s0, (q_in, k_out, v, cd, o_intra))
    return _from_chunks(o, length), s


def _chunk_gla(q, k, v, g, s0, chunk):
    length = q.shape[1]
    q, k, v, g = (_to_chunks(t, chunk) for t in (q, k, v, g))
    b = jnp.cumsum(g, axis=-2)
    diff = b[..., :, None, :] - b[..., None, :, :]
    dec = jnp.exp(jnp.where(_tril(chunk, 0)[:, :, None], diff, -jnp.inf))
    attn = jnp.einsum('nbhtd,nbhjd,nbhtjd->nbhtj', q, k, dec)
    o_intra = jnp.einsum('nbhtj,nbhjv->nbhtv', attn, v)
    q_in = q * jnp.exp(b)
    k_out = k * jnp.exp(b[..., -1:, :] - b)
    cd = jnp.exp(b[..., -1, :])

    def step(s, xs):
        qi_c, ko_c, v_c, cd_c, oi_c = xs
        o = oi_c + jnp.einsum('bhtd,bhdv->bhtv', qi_c, s)
        s = s * cd_c[..., :, None] + jnp.einsum('bhtd,bhtv->bhdv', ko_c, v_c)
        return s, o

    s, o = lax.scan(step, s0, (q_in, k_out, v, cd, o_intra))
    return _from_chunks(o, length), s


def _token_mixers(h, pos, s_a, s_conv, s_b, s_c, w_in, conv_a_w, a_log, dt_bias, norm_a_g,
                  w_lr2, b_lr2, norm_c_g, w_branch, w_out):
    f32 = jnp.float32
    bsz, length, _ = h.shape
    proj = jnp.einsum('bld,de->ble', h, w_in)
    (qkv_a, z_a, beta_a, dec_a, q_b, k_b, v_b, z_b,
     q_c, k_c, v_c, z_c, lr_c, mg) = jnp.split(proj, SPLIT_POINTS, axis=-1)

    qkv_a, new_conv = _causal_dwconv(qkv_a, s_conv, conv_a_w)
    qkv_a = jax.nn.silu(qkv_a.astype(f32))
    qa, ka, va = jnp.split(qkv_a, [HA * DK_A, 2 * HA * DK_A], axis=-1)
    qa = _l2norm(qa.reshape(bsz, length, HA, DK_A)) * (DK_A ** -0.5)
    ka = _l2norm(ka.reshape(bsz, length, HA, DK_A))
    va = va.reshape(bsz, length, HA, DV_A)
    beta = jax.nn.sigmoid(beta_a.astype(f32))
    ga = -jnp.exp(a_log.astype(f32)) * jax.nn.softplus(dec_a.astype(f32) + dt_bias.astype(f32))
    oa, new_a = _chunk_gated_delta(qa, ka, va, beta, ga, s_a.astype(f32), CHUNK_A)
    oa = _rmsnorm(oa, norm_a_g) * jax.nn.silu(z_a.astype(f32)).reshape(bsz, length, HA, DV_A)

    qb = _rope(q_b.astype(f32).reshape(bsz, length, HB, DK_B), pos)
    kb = _rope(k_b.astype(f32).reshape(bsz, length, HB, DK_B), pos) * (DK_B ** -0.5)
    vb = v_b.astype(f32).reshape(bsz, length, HB, DV_B)
    log_gamma = jnp.log1p(-jnp.exp2(-5.0 - jnp.arange(HB, dtype=f32)))
    gb = jnp.broadcast_to(log_gamma, (bsz, length, HB))
    ob, new_b = _chunk_scalar_decay(qb, kb, vb, gb, s_b.astype(f32), CHUNK_B)
    ob = _rmsnorm(ob) * jax.nn.silu(z_b.astype(f32)).reshape(bsz, length, HB, DV_B)

    qc = q_c.astype(f32).reshape(bsz, length, HC, DK_C) * (DK_C ** -0.5)
    kc = k_c.astype(f32).reshape(bsz, length, HC, DK_C)
    vc = v_c.astype(f32).reshape(bsz, length, HC, DV_C)
    gc = jax.nn.log_sigmoid(jnp.einsum('blr,re->ble', lr_c.astype(f32), w_lr2.astype(f32))
                            + b_lr2.astype(f32)) / GLA_TAU
    gc = gc.reshape(bsz, length, HC, DK_C)
    oc, new_c = _chunk_gla(qc, kc, vc, gc, s_c.astype(f32), CHUNK_C)
    oc = _rmsnorm(oc, norm_c_g) * jax.nn.silu(z_c.astype(f32)).reshape(bsz, length, HC, DV_C)

    branches = jnp.stack([oa.reshape(bsz, length, BRANCH_W), ob.reshape(bsz, length, BRANCH_W),
                          oc.reshape(bsz, length, BRANCH_W)], axis=2).astype(h.dtype)
    up = jnp.einsum('blnw,nwd->blnd', branches, w_branch)
    gates = jax.nn.sigmoid(mg.astype(f32)).reshape(bsz, length, N_BRANCH, D_MODEL)
    merged = jnp.sum(gates * up.astype(f32), axis=2).astype(h.dtype)
    out = jnp.einsum('bld,de->ble', merged, w_out)
    return out, new_a, new_conv, new_b, new_c


def _conv_ffn(h, s_f, w_ffn_in, ffn_conv_w, ffn_conv_b, w_ffn_out):
    u = jnp.einsum('bld,de->ble', h, w_ffn_in)
    a, b = jnp.split(u, 2, axis=-1)
    a, new_f = _causal_dwconv(a, s_f, ffn_conv_w)
    act = jax.nn.silu((a + ffn_conv_b).astype(jnp.float32)) * b.astype(jnp.float32)
    return jnp.einsum('blf,fd->bld', act.astype(h.dtype), w_ffn_out), new_f


def _run_group(x, c, pos, st_delta, st_dconv, st_ret, st_gla, st_fconv, weights, final_norm_g):
    (w_ada, b_ada, norm1_g, w_in, conv_a_w, a_log, dt_bias, norm_a_g, w_lr2, b_lr2, norm_c_g,
     w_branch, w_out, norm2_g, w_ffn_in, ffn_conv_w, ffn_conv_b, w_ffn_out) = weights
    n_delta, n_dconv, n_ret, n_gla, n_fconv = [], [], [], [], []
    for l in range(DEPTH):
        mod = (jnp.einsum('bd,de->be', jax.nn.silu(c), w_ada[l]) + b_ada[l]).astype(jnp.float32)
        sh1, sc1, g1, sh2, sc2, g2 = (m[:, None, :] for m in jnp.split(mod, 6, axis=-1))
        h = (_rmsnorm(x, norm1_g[l]) * (1.0 + sc1) + sh1).astype(x.dtype)
        mix, s_a, s_cv, s_b, s_c = _token_mixers(
            h, pos, st_delta[l], st_dconv[l], st_ret[l], st_gla[l], w_in[l], conv_a_w[l], a_log[l],
            dt_bias[l], norm_a_g[l], w_lr2[l], b_lr2[l], norm_c_g[l], w_branch[l], w_out[l])
        x = (x + g1 * mix).astype(x.dtype)
        h = (_rmsnorm(x, norm2_g[l]) * (1.0 + sc2) + sh2).astype(x.dtype)
        ffn, s_f = _conv_ffn(h, st_fconv[l], w_ffn_in[l], ffn_conv_w[l], ffn_conv_b[l], w_ffn_out[l])
        x = (x + g2 * ffn).astype(x.dtype)
        n_delta.append(s_a.astype(st_delta.dtype))
        n_dconv.append(s_cv.astype(st_dconv.dtype))
        n_ret.append(s_b.astype(st_ret.dtype))
        n_gla.append(s_c.astype(st_gla.dtype))
        n_fconv.append(s_f.astype(st_fconv.dtype))
    y = _rmsnorm(x, final_norm_g).astype(x.dtype)
    return y, jnp.stack(n_delta), jnp.stack(n_dconv), jnp.stack(n_ret), jnp.stack(n_gla), jnp.stack(n_fconv)


def setup_inputs(seed: int = 0) -> dict:
    key = jax.random.key(seed)
    ks = jax.random.split(key, 32)
    f32 = jnp.float32

    def nrm(k, shape, scale):
        return jax.random.normal(k, shape, f32) * scale

    dt = jnp.exp(jax.random.uniform(ks[14], (DEPTH, HA), f32, float(np.log(1e-3)), float(np.log(1e-1))))
    return {
        "x_prompt": nrm(ks[0], (BATCH, SEQ, D_MODEL), 1.0),
        "x_sample": nrm(ks[1], (DEC_BATCH, DEC_SEQ, D_MODEL), 1.0),
        "state_delta": nrm(ks[2], (DEPTH, DEC_BATCH, HA, DK_A, DV_A), 0.3),
        "state_delta_conv": nrm(ks[3], (DEPTH, DEC_BATCH, CONV_A - 1, CONV_CH_A), 1.0),
        "state_ret": nrm(ks[4], (DEPTH, DEC_BATCH, HB, DK_B, DV_B), 1.0),
        "state_gla": nrm(ks[5], (DEPTH, DEC_BATCH, HC, DK_C, DV_C), 0.5),
        "state_ffn_conv": nrm(ks[6], (DEPTH, DEC_BATCH, FFN_CONV - 1, D_FF), 1.0),
        "c_prompt": nrm(ks[7], (BATCH, D_MODEL), 1.0),
        "c_sample": nrm(ks[8], (DEC_BATCH, D_MODEL), 1.0),
        "w_ada": nrm(ks[9], (DEPTH, D_MODEL, 6 * D_MODEL), 0.5 * D_MODEL ** -0.5),
        "b_ada": nrm(ks[10], (DEPTH, 6 * D_MODEL), 0.02),
        "norm1_g": 1.0 + nrm(ks[11], (DEPTH, D_MODEL), 0.02),
        "w_in": nrm(ks[12], (DEPTH, D_MODEL, D_IN), D_MODEL ** -0.5),
        "conv_a_w": nrm(ks[13], (DEPTH, CONV_A, CONV_CH_A), CONV_A ** -0.5),
        "a_log": jnp.log(jax.random.uniform(ks[15], (DEPTH, HA), f32, 1.0, 16.0)),
        "dt_bias": dt + jnp.log(-jnp.expm1(-dt)),
        "norm_a_g": 1.0 + nrm(ks[16], (DEPTH, DV_A), 0.02),
        "w_lr2": nrm(ks[17], (DEPTH, GLA_RANK, HC * DK_C), GLA_RANK ** -0.5),
        "b_lr2": nrm(ks[18], (DEPTH, HC * DK_C), 0.1),
        "norm_c_g": 1.0 + nrm(ks[19], (DEPTH, DV_C), 0.02),
        "w_branch": nrm(ks[20], (DEPTH, N_BRANCH, BRANCH_W, D_MODEL), BRANCH_W ** -0.5),
        "w_out": nrm(ks[21], (DEPTH, D_MODEL, D_MODEL), D_MODEL ** -0.5),
        "norm2_g": 1.0 + nrm(ks[22], (DEPTH, D_MODEL), 0.02),
        "w_ffn_in": nrm(ks[23], (DEPTH, D_MODEL, 2 * D_FF), D_MODEL ** -0.5),
        "ffn_conv_w": nrm(ks[24], (DEPTH, FFN_CONV, D_FF), FFN_CONV ** -0.5),
        "ffn_conv_b": nrm(ks[25], (DEPTH, D_FF), 0.02),
        "w_ffn_out": nrm(ks[26], (DEPTH, D_FF, D_MODEL), D_FF ** -0.5),
        "final_norm_g": 1.0 + nrm(ks[27], (D_MODEL,), 0.02),
    }


def reference(x_prompt, x_sample, state_delta, state_delta_conv, state_ret, state_gla, state_ffn_conv,
              c_prompt, c_sample, w_ada, b_ada, norm1_g, w_in, conv_a_w, a_log, dt_bias, norm_a_g,
              w_lr2, b_lr2, norm_c_g, w_branch, w_out, norm2_g, w_ffn_in, ffn_conv_w, ffn_conv_b,
              w_ffn_out, final_norm_g):
    weights = (w_ada, b_ada, norm1_g, w_in, conv_a_w, a_log, dt_bias, norm_a_g, w_lr2, b_lr2, norm_c_g,
               w_branch, w_out, norm2_g, w_ffn_in, ffn_conv_w, ffn_conv_b, w_ffn_out)
    pos_p = jnp.arange(x_prompt.shape[1], dtype=jnp.float32)
    pos_s = PAST_LEN + jnp.arange(x_sample.shape[1], dtype=jnp.float32)
    bp = x_prompt.shape[0]
    dtp = x_prompt.dtype
    y_prompt, p_delta, p_dconv, p_ret, p_gla, p_fconv = _run_group(
        x_prompt, c_prompt, pos_p,
        jnp.zeros((DEPTH, bp, HA, DK_A, DV_A), dtp),
        jnp.zeros((DEPTH, bp, CONV_A - 1, CONV_CH_A), dtp),
        jnp.zeros((DEPTH, bp, HB, DK_B, DV_B), dtp),
        jnp.zeros((DEPTH, bp, HC, DK_C, DV_C), dtp),
        jnp.zeros((DEPTH, bp, FFN_CONV - 1, D_FF), dtp),
        weights, final_norm_g)
    y_sample, s_delta, s_dconv, s_ret, s_gla, s_fconv = _run_group(
        x_sample, c_sample, pos_s, state_delta, state_delta_conv, state_ret, state_gla, state_ffn_conv,
        weights, final_norm_g)
    return (y_prompt, y_sample, p_delta, p_dconv, p_ret, p_gla, p_fconv,
            s_delta, s_dconv, s_ret, s_gla, s_fconv)
```

```python
import functools
import math

import jax
import jax.numpy as jnp
from jax import lax
from jax.experimental import pallas as pl
from jax.experimental.pallas import tpu as pltpu

F32 = jnp.float32
BF16 = jnp.bfloat16
HIGHEST = lax.Precision.HIGHEST

D_MODEL = 2048
DEPTH = 4
BRANCH_W = D_MODEL // 2
N_BRANCH = 3
DK = 128
HA, DV_A = 8, 128
CONV_A = 4
CONV_CH_A = 2 * HA * DK + HA * DV_A
HB, DV_B = 4, 256
HC, DV_C = 4, 256
GLA_RANK = 16
GLA_TAU = 16.0
GLA_SUB = 16
ROPE_BASE = 10000.0
D_FF = ((8 * D_MODEL // 3 + 255) // 256) * 256
FFN_CONV = 3
EPS = 1e-6
CHUNK = 64
CHUNK_SINGLE = 16
LANES = 128
SUBLANES = 8

OFF_QKV_A, OFF_Z_A = 0, 3072
OFF_Q_B, OFF_K_B, OFF_V_B, OFF_Z_B = 4096, 4608, 5120, 6144
OFF_Q_C, OFF_K_C, OFF_V_C, OFF_Z_C = 7168, 7680, 8192, 9216
OFF_MG = 10240
D_MAIN = 16384
SM_BETA, SM_DEC, SM_LR = 0, 8, 16

VMEM_LIMIT = 56 * 1024 * 1024


def _cparams(sem):
    return pltpu.CompilerParams(dimension_semantics=sem, vmem_limit_bytes=VMEM_LIMIT)


def _dot(a, b):
    return jnp.dot(a.astype(BF16), b.astype(BF16), preferred_element_type=F32)


def _dot_nt(a, b):
    return lax.dot_general(a.astype(BF16), b.astype(BF16), (((1,), (1,)), ((), ())),
                           preferred_element_type=F32)


def _dot_tn(a, b):
    return lax.dot_general(a.astype(BF16), b.astype(BF16), (((0,), (0,)), ((), ())),
                           preferred_element_type=F32)


def _dot_hi(a, b):
    return jnp.dot(a, b, precision=HIGHEST, preferred_element_type=F32)


def _dot_tn_hi(a, b):
    return lax.dot_general(a, b, (((0,), (0,)), ((), ())), precision=HIGHEST,
                           preferred_element_type=F32)


def _sigmoid(x):
    return 1.0 / (1.0 + jnp.exp(-x))


def _silu(x):
    return x * _sigmoid(x)


def _softplus(x):
    return jnp.maximum(x, 0.0) + jnp.log1p(jnp.exp(-jnp.abs(x)))


def _rms(x):
    return x * lax.rsqrt(jnp.mean(x * x, axis=-1, keepdims=True) + EPS)


def _mm_kernel(*refs, has_bias, has_res, silu_in):
    x_ref, w_ref = refs[0], refs[1]
    o_ref = refs[-1]
    x = x_ref[...]
    if silu_in:
        x = _silu(x.astype(F32))
    acc = _dot(x, w_ref[...])
    k = 2
    if has_bias:
        acc = acc + refs[k][...]
        k += 1
    if has_res:
        acc = refs[k][...] + refs[k + 1][...] * acc
    o_ref[...] = acc.astype(o_ref.dtype)


def _matmul(x, w, *, tm, tn, out_dtype, res=None, gate=None, gate_col=0, rows_per_gate=None):
    m, kdim = x.shape
    n = w.shape[1]
    assert m % tm == 0 and n % tn == 0
    in_specs = [pl.BlockSpec((tm, kdim), lambda i, j: (i, 0)),
                pl.BlockSpec((kdim, tn), lambda i, j: (0, j))]
    args = [x, w]
    if res is not None:
        tiles_per_gate = rows_per_gate // tm
        gcol = gate_col // tn
        in_specs += [pl.BlockSpec((tm, tn), lambda i, j: (i, j)),
                     pl.BlockSpec((None, gate.shape[1], tn),
                                  lambda i, j: (i // tiles_per_gate, 0, gcol + j))]
        args += [res, gate]
    return pl.pallas_call(
        functools.partial(_mm_kernel, has_bias=False, has_res=res is not None, silu_in=False),
        grid=(m // tm, n // tn),
        in_specs=in_specs,
        out_specs=pl.BlockSpec((tm, tn), lambda i, j: (i, j)),
        out_shape=jax.ShapeDtypeStruct((m, n), out_dtype),
        compiler_params=_cparams(("parallel", "parallel")),
    )(*args)


def _ada_all(c_all, w_ada, b_ada):
    m = c_all.shape[0]
    n = w_ada.shape[2]
    tn = 1024
    return pl.pallas_call(
        functools.partial(_mm_kernel, has_bias=True, has_res=False, silu_in=True),
        grid=(DEPTH, n // tn),
        in_specs=[pl.BlockSpec((m, D_MODEL), lambda l, j: (0, 0)),
                  pl.BlockSpec((None, D_MODEL, tn), lambda l, j: (l, 0, j)),
                  pl.BlockSpec((None, 1, tn), lambda l, j: (l, 0, j))],
        out_specs=pl.BlockSpec((None, m, tn), lambda l, j: (l, 0, j)),
        out_shape=jax.ShapeDtypeStruct((DEPTH, m, n), F32),
        compiler_params=_cparams(("parallel", "parallel")),
    )(c_all, w_ada, b_ada.reshape(DEPTH, 1, n))


def _norm_kernel(*refs, modulated):
    x_ref, g_ref = refs[0], refs[1]
    o_ref = refs[-1]
    y = _rms(x_ref[...]) * g_ref[...]
    if modulated:
        y = y * (1.0 + refs[2][...]) + refs[3][...]
    o_ref[...] = y.astype(o_ref.dtype)


def _norm(x, g, *, tm, out_dtype, mod=None, scale_col=0, shift_col=0, rows_per_gate=None):
    m = x.shape[0]
    in_specs = [pl.BlockSpec((tm, D_MODEL), lambda i: (i, 0)),
                pl.BlockSpec((1, D_MODEL), lambda i: (0, 0))]
    args = [x, g.reshape(1, D_MODEL)]
    if mod is not None:
        tiles_per_gate = rows_per_gate // tm
        for col in (scale_col, shift_col):
            cb = col // D_MODEL
            in_specs.append(pl.BlockSpec((None, mod.shape[1], D_MODEL),
                                         lambda i, cb=cb: (i // tiles_per_gate, 0, cb)))
            args.append(mod)
    return pl.pallas_call(
        functools.partial(_norm_kernel, modulated=mod is not None),
        grid=(m // tm,),
        in_specs=in_specs,
        out_specs=pl.BlockSpec((tm, D_MODEL), lambda i: (i, 0)),
        out_shape=jax.ShapeDtypeStruct((m, D_MODEL), out_dtype),
        compiler_params=_cparams(("parallel",)),
    )(*args)


def _load_rows(ref, bi, cols, rows, single):
    if single:
        return jnp.broadcast_to(ref[bi, :, cols], (rows, cols.stop - cols.start))
    return ref[bi, :, cols]


def _store_rows(ref, bi, cols, val, single):
    if single:
        ref[bi, :, cols] = val[0:1].astype(ref.dtype)
    else:
        ref[bi, :, cols] = val.astype(ref.dtype)


def _mixer_a_kernel(qkv_ref, z_ref, sm_ref, cw_ref, alog_ref, dtb_ref, ng_ref, s0_ref, cv0_ref,
                    o_ref, s_ref, xbuf, *, C, bb, single):
    c = pl.program_id(1)

    @pl.when(c == 0)
    def _():
        s_ref[...] = s0_ref[...]
        xbuf[:, 0:SUBLANES, :] = cv0_ref[...]

    row = lax.broadcasted_iota(jnp.int32, (C, 1), 0)
    rr = lax.broadcasted_iota(jnp.int32, (C, C), 0)
    cc = lax.broadcasted_iota(jnp.int32, (C, C), 1)
    lane = lax.broadcasted_iota(jnp.int32, (1, LANES), 1)
    tril = (rr >= cc).astype(F32)
    triu = (rr <= cc).astype(F32)
    eye = (rr == cc).astype(F32)
    is_dec = jnp.logical_and(lane >= SM_DEC, lane < SM_DEC + HA)
    tail0 = SUBLANES - (CONV_A - 1)

    for bi in range(bb):
        if single:
            xbuf[bi, SUBLANES:SUBLANES + C, :] = jnp.zeros((C, CONV_CH_A), F32)
            xbuf[bi, SUBLANES:SUBLANES + 1, :] = qkv_ref[bi]
        else:
            xbuf[bi, SUBLANES:SUBLANES + C, :] = qkv_ref[bi]

        def conv(col0, bi=bi):
            cols = slice(col0, col0 + LANES)
            acc = cw_ref[0:1, cols] * xbuf[bi, tail0:tail0 + C, cols]
            for i in range(1, CONV_A):
                acc = acc + cw_ref[i:i + 1, cols] * xbuf[bi, tail0 + i:tail0 + i + C, cols]
            return _silu(acc)

        sm = _load_rows(sm_ref, bi, slice(0, LANES), C, single)
        gfull = jnp.where(is_dec, -jnp.exp(alog_ref[...]) * _softplus(sm + dtb_ref[...]), 0.0)
        beta_full = _sigmoid(sm)
        if single:
            gfull = jnp.where(row == 0, gfull, 0.0)
            beta_full = jnp.where(row == 0, beta_full, 0.0)
        b_col = _dot_hi(tril, gfull)
        b_row = _dot_tn_hi(gfull, triu)

        for h in range(HA):
            bc = b_col[:, SM_DEC + h:SM_DEC + h + 1]
            br = b_row[SM_DEC + h:SM_DEC + h + 1, :]
            e = jnp.exp(jnp.minimum(bc - br, 0.0))
            d_incl = jnp.where(rr >= cc, e, 0.0)
            beta = beta_full[:, SM_BETA + h:SM_BETA + h + 1]
            eb = jnp.exp(bc)
            b_last = bc[C - 1:C, :]
            q = conv(h * DK)
            k = conv(HA * DK + h * DK)
            v = conv(2 * HA * DK + h * DV_A)
            q = q * lax.rsqrt(jnp.sum(q * q, axis=-1, keepdims=True) + EPS) * (DK ** -0.5)
            k = k * lax.rsqrt(jnp.sum(k * k, axis=-1, keepdims=True) + EPS)
            qk = _dot_nt(q, k)
            u_v = beta * v
            w = (beta * eb) * k
            if not single:
                kk = _dot_nt(k, k)
                nmat = beta * kk * jnp.where(rr > cc, e, 0.0)
                x_inv = eye - nmat
                n_pow = _dot_hi(nmat, nmat)
                n_lev = int(math.log2(C)) - 1
                for lev in range(n_lev):
                    x_inv = x_inv + _dot_hi(x_inv, n_pow)
                    if lev < n_lev - 1:
                        n_pow = _dot_hi(n_pow, n_pow)
                u_v = _dot_hi(x_inv, u_v)
                w = _dot_hi(x_inv, w)
            s = s_ref[bi, h]
            u = u_v - _dot(w, s)
            o = _dot(q * eb, s) + _dot(qk * d_incl, u)
            s_ref[bi, h] = s * jnp.exp(b_last) + _dot_tn(k * jnp.exp(b_last - bc), u)
            zc = slice(h * DV_A, (h + 1) * DV_A)
            z = _load_rows(z_ref, bi, zc, C, single)
            _store_rows(o_ref, bi, zc, _rms(o) * ng_ref[...] * _silu(z), single)

        if not single:
            xbuf[bi, tail0:SUBLANES, :] = xbuf[bi, C + tail0:C + SUBLANES, :]


def _row_pad(vec, offset):
    return jnp.zeros((1, LANES), F32).at[0, offset:offset + vec.shape[0]].set(vec.astype(F32))


def _seq_blocks(single, bb, C, width, col_block):
    rows = 1 if single else C
    return pl.BlockSpec((bb, rows, width), lambda b, c: (b, c, col_block))


def _mixer_a(proj, small, s0, conv_buf, conv_w, a_log, dt_bias, norm_g, *, single):
    bsz, length, _ = proj.shape
    C = CHUNK_SINGLE if single else CHUNK
    bb = 8 if single else 1
    n_chunks = 1 if single else length // C
    assert (length == 1) if single else (length % C == 0)
    cv0 = jnp.pad(conv_buf.astype(F32), ((0, 0), (SUBLANES - (CONV_A - 1), 0), (0, 0)))
    full = lambda shape: pl.BlockSpec(shape, lambda b, c: (0,) * len(shape))
    out, s_new = pl.pallas_call(
        functools.partial(_mixer_a_kernel, C=C, bb=bb, single=single),
        grid=(bsz // bb, n_chunks),
        in_specs=[_seq_blocks(single, bb, C, CONV_CH_A, OFF_QKV_A // CONV_CH_A),
                  _seq_blocks(single, bb, C, HA * DV_A, OFF_Z_A // (HA * DV_A)),
                  _seq_blocks(single, bb, C, LANES, 0),
                  full((CONV_A, CONV_CH_A)), full((1, LANES)), full((1, LANES)), full((1, DV_A)),
                  pl.BlockSpec((bb, HA, DK, DV_A), lambda b, c: (b, 0, 0, 0)),
                  pl.BlockSpec((bb, SUBLANES, CONV_CH_A), lambda b, c: (b, 0, 0))],
        out_specs=[_seq_blocks(single, bb, C, HA * DV_A, 0),
                   pl.BlockSpec((bb, HA, DK, DV_A), lambda b, c: (b, 0, 0, 0))],
        out_shape=[jax.ShapeDtypeStruct((bsz, length, HA * DV_A), F32 if single else BF16),
                   jax.ShapeDtypeStruct((bsz, HA, DK, DV_A), F32)],
        scratch_shapes=[pltpu.VMEM((bb, C + SUBLANES, CONV_CH_A), F32)],
        compiler_params=_cparams(("parallel", "arbitrary")),
    )(proj, proj, small, conv_w.astype(F32), _row_pad(a_log, SM_DEC), _row_pad(dt_bias, SM_DEC),
      norm_g.reshape(1, DV_A).astype(F32), s0.astype(F32), cv0)
    return out, s_new


def _mixer_b_kernel(q_ref, k_ref, v_ref, z_ref, cos_ref, sin_ref, s0_ref, o_ref, s_ref,
                    *, C, bb, single):
    c = pl.program_id(1)

    @pl.when(c == 0)
    def _():
        s_ref[...] = s0_ref[...]

    row = lax.broadcasted_iota(jnp.int32, (C, 1), 0)
    rr = lax.broadcasted_iota(jnp.int32, (C, C), 0)
    cc = lax.broadcasted_iota(jnp.int32, (C, C), 1)
    if single:
        cosf = jnp.broadcast_to(cos_ref[...], (C, DK))
        sinf = jnp.broadcast_to(sin_ref[...], (C, DK))
        steps_col = jnp.ones((C, 1), F32)
        steps_mat = jnp.zeros((C, C), F32)
    else:
        cosf, sinf = cos_ref[...], sin_ref[...]
        steps_col = (row + 1).astype(F32)
        steps_mat = (rr - cc).astype(F32)
    steps_last = steps_col[C - 1:C, :]

    def rope(x):
        return x * cosf + pltpu.roll(x, DK // 2, 1) * sinf

    for bi in range(bb):
        for h in range(HB):
            log_gamma = math.log1p(-2.0 ** (-5.0 - h))
            kc = slice(h * DK, (h + 1) * DK)
            vc = slice(h * DV_B, (h + 1) * DV_B)
            q = rope(_load_rows(q_ref, bi, kc, C, single))
            k = rope(_load_rows(k_ref, bi, kc, C, single)) * (DK ** -0.5)
            v = _load_rows(v_ref, bi, vc, C, single)
            if single:
                v = jnp.where(row == 0, v, 0.0)
            dec = jnp.where(rr >= cc, jnp.exp(jnp.minimum(steps_mat * log_gamma, 0.0)), 0.0)
            s = s_ref[bi, h]
            o = _dot(_dot_nt(q, k) * dec, v) + _dot(q * jnp.exp(steps_col * log_gamma), s)
            k_out = k * jnp.exp((steps_last - steps_col) * log_gamma)
            s_ref[bi, h] = s * jnp.exp(steps_last * log_gamma) + _dot_tn(k_out, v)
            z = _load_rows(z_ref, bi, vc, C, single)
            _store_rows(o_ref, bi, vc, _rms(o) * _silu(z), single)


def _mixer_b(proj, s0, cos_t, sin_t, *, single):
    bsz, length, _ = proj.shape
    C = CHUNK_SINGLE if single else CHUNK
    bb = 8 if single else 1
    n_chunks = 1 if single else length // C
    rows = 1 if single else C
    tab = pl.BlockSpec((rows, DK), lambda b, c: (c, 0))
    out, s_new = pl.pallas_call(
        functools.partial(_mixer_b_kernel, C=C, bb=bb, single=single),
        grid=(bsz // bb, n_chunks),
        in_specs=[_seq_blocks(single, bb, C, HB * DK, OFF_Q_B // (HB * DK)),
                  _seq_blocks(single, bb, C, HB * DK, OFF_K_B // (HB * DK)),
                  _seq_blocks(single, bb, C, HB * DV_B, OFF_V_B // (HB * DV_B)),
                  _seq_blocks(single, bb, C, HB * DV_B, OFF_Z_B // (HB * DV_B)),
                  tab, tab,
                  pl.BlockSpec((bb, HB, DK, DV_B), lambda b, c: (b, 0, 0, 0))],
        out_specs=[_seq_blocks(single, bb, C, HB * DV_B, 0),
                   pl.BlockSpec((bb, HB, DK, DV_B), lambda b, c: (b, 0, 0, 0))],
        out_shape=[jax.ShapeDtypeStruct((bsz, length, HB * DV_B), F32 if single else BF16),
                   jax.ShapeDtypeStruct((bsz, HB, DK, DV_B), F32)],
        compiler_params=_cparams(("parallel", "arbitrary")),
    )(proj, proj, proj, proj, cos_t, sin_t, s0.astype(F32))
    return out, s_new


def _mixer_c_kernel(q_ref, k_ref, v_ref, z_ref, sm_ref, w2_ref, b2_ref, ng_ref, s0_ref,
                    o_ref, s_ref, *, C, bb, single):
    c = pl.program_id(1)

    @pl.when(c == 0)
    def _():
        s_ref[...] = s0_ref[...]

    sub = GLA_SUB
    row = lax.broadcasted_iota(jnp.int32, (C, 1), 0)
    rr = lax.broadcasted_iota(jnp.int32, (C, C), 0)
    cc = lax.broadcasted_iota(jnp.int32, (C, C), 1)
    blk_tril = jnp.logical_and(rr >= cc, (rr // sub) == (cc // sub)).astype(F32)
    r16 = lax.broadcasted_iota(jnp.int32, (sub, sub), 0)
    c16 = lax.broadcasted_iota(jnp.int32, (sub, sub), 1)
    ones_sub = jnp.ones((sub, DK), F32)

    for bi in range(bb):
        sm = _load_rows(sm_ref, bi, slice(0, LANES), C, single)
        x = _dot(sm, w2_ref[...]) + b2_ref[...]
        gc = (jnp.minimum(x, 0.0) - jnp.log1p(jnp.exp(-jnp.abs(x)))) * (1.0 / GLA_TAU)
        if single:
            gc = jnp.where(row == 0, gc, 0.0)
        bcum = _dot_hi(blk_tril, gc)
        for sc in range(C // sub):
            rs = slice(sc * sub, (sc + 1) * sub)
            for h in range(HC):
                kc = slice(h * DK, (h + 1) * DK)
                vc = slice(h * DV_C, (h + 1) * DV_C)
                if single:
                    q = jnp.broadcast_to(q_ref[bi, :, kc], (sub, DK))
                    k = jnp.broadcast_to(k_ref[bi, :, kc], (sub, DK))
                    v = jnp.where(row == 0, jnp.broadcast_to(v_ref[bi, :, vc], (sub, DV_C)), 0.0)
                    z = jnp.broadcast_to(z_ref[bi, :, vc], (sub, DV_C))
                else:
                    q, k = q_ref[bi, rs, kc], k_ref[bi, rs, kc]
                    v, z = v_ref[bi, rs, vc], z_ref[bi, rs, vc]
                b = bcum[rs, kc]
                g = gc[rs, kc]
                b_last = b[sub - 1:sub, :]
                qi = q * (DK ** -0.5) * jnp.exp(b)
                attn = jnp.where(r16 >= c16, _dot_nt(qi, k * jnp.exp(-b)), 0.0)
                s = s_ref[bi, h]
                o = _dot(attn, v) + _dot(qi, s)
                cd = jnp.exp(_dot_tn_hi(g, ones_sub))
                s_ref[bi, h] = (s * jnp.concatenate([cd, cd], axis=1)
                                + _dot_tn(k * jnp.exp(b_last - b), v))
                o_n = _rms(o) * ng_ref[...] * _silu(z)
                if single:
                    o_ref[bi, :, vc] = o_n[0:1].astype(o_ref.dtype)
                else:
                    o_ref[bi, rs, vc] = o_n.astype(o_ref.dtype)


def _mixer_c(proj, small, s0, w_lr2, b_lr2, norm_g, *, single):
    bsz, length, _ = proj.shape
    C = CHUNK_SINGLE if single else CHUNK
    bb = 8 if single else 1
    n_chunks = 1 if single else length // C
    w2 = jnp.zeros((LANES, HC * DK), BF16).at[SM_LR:SM_LR + GLA_RANK].set(w_lr2.astype(BF16))
    full = lambda shape: pl.BlockSpec(shape, lambda b, c: (0,) * len(shape))
    out, s_new = pl.pallas_call(
        functools.partial(_mixer_c_kernel, C=C, bb=bb, single=single),
        grid=(bsz // bb, n_chunks),
        in_specs=[_seq_blocks(single, bb, C, HC * DK, OFF_Q_C // (HC * DK)),
                  _seq_blocks(single, bb, C, HC * DK, OFF_K_C // (HC * DK)),
                  _seq_blocks(single, bb, C, HC * DV_C, OFF_V_C // (HC * DV_C)),
                  _seq_blocks(single, bb, C, HC * DV_C, OFF_Z_C // (HC * DV_C)),
                  _seq_blocks(single, bb, C, LANES, 0),
                  full((LANES, HC * DK)), full((1, HC * DK)), full((1, DV_C)),
                  pl.BlockSpec((bb, HC, DK, DV_C), lambda b, c: (b, 0, 0, 0))],
        out_specs=[_seq_blocks(single, bb, C, HC * DV_C, 0),
                   pl.BlockSpec((bb, HC, DK, DV_C), lambda b, c: (b, 0, 0, 0))],
        out_shape=[jax.ShapeDtypeStruct((bsz, length, HC * DV_C), F32 if single else BF16),
                   jax.ShapeDtypeStruct((bsz, HC, DK, DV_C), F32)],
        compiler_params=_cparams(("parallel", "arbitrary")),
    )(proj, proj, proj, proj, small, w2, b_lr2.reshape(1, HC * DK).astype(F32),
      norm_g.reshape(1, DV_C).astype(F32), s0.astype(F32))
    return out, s_new


def _merge_kernel(ba_ref, bb_ref, bc_ref, wb_ref, ga_ref, gb_ref, gc_ref, o_ref):
    acc = _sigmoid(ga_ref[...]) * _dot(ba_ref[...], wb_ref[0])
    acc = acc + _sigmoid(gb_ref[...]) * _dot(bb_ref[...], wb_ref[1])
    acc = acc + _sigmoid(gc_ref[...]) * _dot(bc_ref[...], wb_ref[2])
    o_ref[...] = acc.astype(o_ref.dtype)


def _merge(br_a, br_b, br_c, w_branch, proj, *, tm, tn):
    m = br_a.shape[0]
    br = pl.BlockSpec((tm, BRANCH_W), lambda i, j: (i, 0))
    gate = lambda n: pl.BlockSpec((tm, tn), lambda i, j: (i, (OFF_MG + n * D_MODEL) // tn + j))
    return pl.pallas_call(
        _merge_kernel,
        grid=(m // tm, D_MODEL // tn),
        in_specs=[br, br, br,
                  pl.BlockSpec((N_BRANCH, BRANCH_W, tn), lambda i, j: (0, 0, j)),
                  gate(0), gate(1), gate(2)],
        out_specs=pl.BlockSpec((tm, tn), lambda i, j: (i, j)),
        out_shape=jax.ShapeDtypeStruct((m, D_MODEL), BF16),
        compiler_params=_cparams(("parallel", "parallel")),
    )(br_a, br_b, br_c, w_branch, proj, proj, proj)


def _ffn_act_kernel(a_ref, p_ref, b_ref, w_ref, bias_ref, o_ref, abuf, *, tm, tiles_per_seq):
    i = pl.program_id(0)
    first = (i % tiles_per_seq) == 0
    abuf[0:SUBLANES, :] = jnp.where(first, 0.0, p_ref[...])
    abuf[SUBLANES:, :] = a_ref[...]
    t0 = SUBLANES - (FFN_CONV - 1)
    y = bias_ref[...] + w_ref[0:1, :] * abuf[t0:t0 + tm, :]
    for k in range(1, FFN_CONV):
        y = y + w_ref[k:k + 1, :] * abuf[t0 + k:t0 + k + tm, :]
    o_ref[...] = (_silu(y) * b_ref[...]).astype(o_ref.dtype)


def _ffn_act(u, conv_w, conv_b, *, seq_len, tm, tn):
    m = u.shape[0]
    nb = D_FF // tn
    sub_per_tile = tm // SUBLANES
    return pl.pallas_call(
        functools.partial(_ffn_act_kernel, tm=tm, tiles_per_seq=seq_len // tm),
        grid=(m // tm, nb),
        in_specs=[pl.BlockSpec((tm, tn), lambda i, j: (i, j)),
                  pl.BlockSpec((SUBLANES, tn),
                               lambda i, j: (jnp.maximum(i * sub_per_tile - 1, 0), j)),
                  pl.BlockSpec((tm, tn), lambda i, j: (i, nb + j)),
                  pl.BlockSpec((FFN_CONV, tn), lambda i, j: (0, j)),
                  pl.BlockSpec((1, tn), lambda i, j: (0, j))],
        out_specs=pl.BlockSpec((tm, tn), lambda i, j: (i, j)),
        out_shape=jax.ShapeDtypeStruct((m, D_FF), BF16),
        scratch_shapes=[pltpu.VMEM((tm + SUBLANES, tn), F32)],
        compiler_params=_cparams(("parallel", "parallel")),
    )(u, u, u, conv_w.astype(F32), conv_b.reshape(1, D_FF).astype(F32))


def _ffn_act_step_kernel(a_ref, b_ref, p2_ref, p1_ref, w_ref, bias_ref, o_ref):
    y = (bias_ref[...] + w_ref[0:1, :] * p2_ref[...] + w_ref[1:2, :] * p1_ref[...]
         + w_ref[2:3, :] * a_ref[...])
    o_ref[...] = (_silu(y) * b_ref[...]).astype(o_ref.dtype)


def _ffn_act_step(u, buf, conv_w, conv_b, *, tn):
    m = u.shape[0]
    nb = D_FF // tn
    blk = lambda off: pl.BlockSpec((m, tn), lambda j: (0, off + j))
    return pl.pallas_call(
        _ffn_act_step_kernel,
        grid=(nb,),
        in_specs=[blk(0), blk(nb), blk(0), blk(0),
                  pl.BlockSpec((FFN_CONV, tn), lambda j: (0, j)),
                  pl.BlockSpec((1, tn), lambda j: (0, j))],
        out_specs=blk(0),
        out_shape=jax.ShapeDtypeStruct((m, D_FF), BF16),
        compiler_params=_cparams(("parallel",)),
    )(u, u, buf[:, 0].astype(F32), buf[:, 1].astype(F32), conv_w.astype(F32),
      conv_b.reshape(1, D_FF).astype(F32))


def _prep_w_in(w_in):
    main = jnp.concatenate([w_in[:, 0:4096], w_in[:, 4112:10256], w_in[:, 10272:16416]], axis=1)
    small = jnp.concatenate([w_in[:, 4096:4112], w_in[:, 10256:10272],
                             jnp.zeros((D_MODEL, LANES - 32), w_in.dtype)], axis=1)
    return main.astype(BF16), small.astype(BF16)


def _rope_tables(pos):
    half = DK // 2
    inv = ROPE_BASE ** (-jnp.arange(half, dtype=F32) / half)
    ang = pos[:, None] * inv[None, :]
    cos, sin = jnp.cos(ang), jnp.sin(ang)
    return jnp.concatenate([cos, cos], axis=1), jnp.concatenate([-sin, sin], axis=1)


def _run_group(x, mod, pos, states, wts, final_norm_g):
    bsz, length, _ = x.shape
    single = length == 1
    m = bsz * length
    st_delta, st_dconv, st_ret, st_gla, st_fconv = states
    cos_t, sin_t = _rope_tables(pos)
    tm = m if single else min(length, 1024)
    tm_norm = m if single else min(length, 512)
    rows_per_gate = m if single else length
    xf = x.reshape(m, D_MODEL)
    outs = ([], [], [], [], [])
    for l in range(DEPTH):
        w = wts[l]
        gate = mod[l]
        h = _norm(xf, w["norm1_g"], tm=tm_norm, out_dtype=BF16, mod=gate, scale_col=D_MODEL,
                  shift_col=0, rows_per_gate=rows_per_gate)
        proj = _matmul(h, w["w_in_main"], tm=tm, tn=1024, out_dtype=F32)
        small = _matmul(h, w["w_in_small"], tm=tm, tn=LANES, out_dtype=F32)
        proj3 = proj.reshape(bsz, length, D_MAIN)
        small3 = small.reshape(bsz, length, LANES)
        br_a, s_a = _mixer_a(proj3, small3, st_delta[l], st_dconv[l], w["conv_a_w"], w["a_log"],
                             w["dt_bias"], w["norm_a_g"], single=single)
        br_b, s_b = _mixer_b(proj3, st_ret[l], cos_t, sin_t, single=single)
        br_c, s_c = _mixer_c(proj3, small3, st_gla[l], w["w_lr2"], w["b_lr2"], w["norm_c_g"],
                             single=single)
        merged = _merge(br_a.reshape(m, BRANCH_W), br_b.reshape(m, BRANCH_W),
                        br_c.reshape(m, BRANCH_W), w["w_branch"], proj,
                        tm=tm_norm, tn=1024)
        xf = _matmul(merged, w["w_out"], tm=tm, tn=1024, out_dtype=F32, res=xf, gate=gate,
                     gate_col=2 * D_MODEL, rows_per_gate=rows_per_gate)
        h = _norm(xf, w["norm2_g"], tm=tm_norm, out_dtype=BF16, mod=gate, scale_col=4 * D_MODEL,
                  shift_col=3 * D_MODEL, rows_per_gate=rows_per_gate)
        u = _matmul(h, w["w_ffn_in"], tm=tm, tn=1024, out_dtype=F32)
        if single:
            act = _ffn_act_step(u, st_fconv[l], w["ffn_conv_w"], w["ffn_conv_b"], tn=512)
        else:
            act = _ffn_act(u, w["ffn_conv_w"], w["ffn_conv_b"], seq_len=length,
                           tm=min(length, 512), tn=512)
        xf = _matmul(act, w["w_ffn_out"], tm=tm, tn=512, out_dtype=F32, res=xf, gate=gate,
                     gate_col=5 * D_MODEL, rows_per_gate=rows_per_gate)
        qkv_rows = proj3[:, :, OFF_QKV_A:OFF_QKV_A + CONV_CH_A]
        a_rows = u.reshape(bsz, length, 2 * D_FF)[:, :, :D_FF]
        new_dconv = jnp.concatenate([st_dconv[l].astype(F32), qkv_rows], axis=1)[:, -(CONV_A - 1):]
        new_fconv = jnp.concatenate([st_fconv[l].astype(F32), a_rows], axis=1)[:, -(FFN_CONV - 1):]
        for acc, val in zip(outs, (s_a, new_dconv, s_b, s_c, new_fconv)):
            acc.append(val)
    y = _norm(xf, final_norm_g, tm=tm_norm, out_dtype=F32).reshape(bsz, length, D_MODEL)
    return (y,) + tuple(jnp.stack(o) for o in outs)


def kernel(x_prompt, x_sample, state_delta, state_delta_conv, state_ret, state_gla, state_ffn_conv, c_prompt, c_sample, w_ada, b_ada, norm1_g, w_in, conv_a_w, a_log, dt_bias, norm_a_g, w_lr2, b_lr2, norm_c_g, w_branch, w_out, norm2_g, w_ffn_in, ffn_conv_w, ffn_conv_b, w_ffn_out, final_norm_g):
    bp, lp, _ = x_prompt.shape
    bs, ls, _ = x_sample.shape
    dtp = x_prompt.dtype
    past_len = 16384

    wts = []
    for l in range(DEPTH):
        main, small = _prep_w_in(w_in[l])
        wts.append(dict(
            norm1_g=norm1_g[l], w_in_main=main, w_in_small=small, conv_a_w=conv_a_w[l],
            a_log=a_log[l], dt_bias=dt_bias[l], norm_a_g=norm_a_g[l], w_lr2=w_lr2[l],
            b_lr2=b_lr2[l], norm_c_g=norm_c_g[l], w_branch=w_branch[l].astype(BF16),
            w_out=w_out[l].astype(BF16), norm2_g=norm2_g[l], w_ffn_in=w_ffn_in[l].astype(BF16),
            ffn_conv_w=ffn_conv_w[l], ffn_conv_b=ffn_conv_b[l],
            w_ffn_out=w_ffn_out[l].astype(BF16)))

    pad_rows = SUBLANES - bp % SUBLANES if bp % SUBLANES else 0
    c_all = jnp.concatenate([c_prompt, jnp.zeros((pad_rows, D_MODEL), c_prompt.dtype), c_sample])
    mod = _ada_all(c_all.astype(F32), w_ada, b_ada)
    mod_p = mod[:, :bp].reshape(DEPTH, bp, 1, 6 * D_MODEL)
    mod_s = mod[:, bp + pad_rows:].reshape(DEPTH, 1, bs, 6 * D_MODEL)

    zeros = lambda *shape: jnp.zeros((DEPTH, bp) + shape, dtp)
    pos_p = jnp.arange(lp, dtype=F32)
    pos_s = past_len + jnp.arange(ls, dtype=F32)
    out_p = _run_group(
        x_prompt, mod_p, pos_p,
        (zeros(HA, DK, DV_A), zeros(CONV_A - 1, CONV_CH_A), zeros(HB, DK, DV_B),
         zeros(HC, DK, DV_C), zeros(FFN_CONV - 1, D_FF)),
        wts, final_norm_g)
    out_s = _run_group(
        x_sample, mod_s, pos_s,
        (state_delta, state_delta_conv, state_ret, state_gla, state_ffn_conv),
        wts, final_norm_g)
    return (out_p[0], out_s[0]) + out_p[1:] + out_s[1:]
```
